```python
import math
import jax
import jax.numpy as jnp
from jax import lax
import numpy as np

D_MODEL = 2048
BATCH = 4
SEQ = 4096
DEPTH = 2

GRID_W = 64
CTX_LEN = 256
N_EVEN = (DEPTH + 1) // 2
N_ODD = DEPTH // 2
N_MOD = 6
SSM_WIDTH = D_MODEL // 2
SSM_GROUP = 16
SSM_GROUPS = SSM_WIDTH // SSM_GROUP
SSM_STATE = 64
N_DIR = 2
DT_MIN = 1e-3
DT_MAX = 1e-1
NA_HEAD_DIM = 128
NA_WIDTH = D_MODEL // 2
NA_HEADS = NA_WIDTH // NA_HEAD_DIM
NA_WIN_R = 8
NA_WIN_C = 16
IN_WIDTH = SSM_WIDTH + 3 * NA_WIDTH
MIX_OUT = SSM_WIDTH + NA_WIDTH
CONV_WIDTH = D_MODEL
CONV_K = 31
FFN_DIM = 5632
FFN_CONV_K = 3
EPS = 1e-6

kernel_name = 'hybrid_s5_natten_conformer_dit'


def rms_norm(x, g):
    xf = x.astype(jnp.float32)
    y = xf * lax.rsqrt(jnp.mean(jnp.square(xf), axis=-1, keepdims=True) + EPS)
    return (y * g.astype(jnp.float32)).astype(x.dtype)


def layer_norm(x, g, b):
    xf = x.astype(jnp.float32)
    mu = jnp.mean(xf, axis=-1, keepdims=True)
    var = jnp.mean(jnp.square(xf - mu), axis=-1, keepdims=True)
    y = (xf - mu) * lax.rsqrt(var + EPS)
    return (y * g.astype(jnp.float32) + b.astype(jnp.float32)).astype(x.dtype)


def modulate(h, shift, scale):
    return h * (1.0 + scale) + shift


def dwconv1d(x, w, b):
    k = w.shape[0]
    pad = (k - 1) // 2
    y = lax.conv_general_dilated(
        x, w[:, None, :].astype(x.dtype), window_strides=(1,), padding=[(pad, pad)],
        dimension_numbers=('NWC', 'WIO', 'NWC'), feature_group_count=x.shape[-1])
    return y + b.astype(x.dtype)


def s5_discretize(lam_re, lam_im, log_dt, b_re, b_im):
    lam_re = lam_re.astype(jnp.float32)
    lam_im = lam_im.astype(jnp.float32)
    b_re = b_re.astype(jnp.float32)
    b_im = b_im.astype(jnp.float32)
    dt = jnp.exp(log_dt.astype(jnp.float32))[:, None]
    mag = jnp.exp(lam_re * dt)
    a_re = mag * jnp.cos(lam_im * dt)
    a_im = mag * jnp.sin(lam_im * dt)
    den = jnp.square(lam_re) + jnp.square(lam_im)
    f_re = ((a_re - 1.0) * lam_re + a_im * lam_im) / den
    f_im = (a_im * lam_re - (a_re - 1.0) * lam_im) / den
    bb_re = f_re[..., None] * b_re - f_im[..., None] * b_im
    bb_im = f_re[..., None] * b_im + f_im[..., None] * b_re
    return a_re, a_im, bb_re, bb_im


def _complex_scan_op(left, right):
    a1r, a1i, h1r, h1i = left
    a2r, a2i, h2r, h2i = right
    return (a2r * a1r - a2i * a1i, a2r * a1i + a2i * a1r,
            a2r * h1r - a2i * h1i + h2r, a2r * h1i + a2i * h1r + h2i)


def s5_states(u, a_re, a_im, bb_re, bb_im, h0=None):
    x_re = jnp.einsum('blgc,gpc->blgp', u, bb_re)
    x_im = jnp.einsum('blgc,gpc->blgp', u, bb_im)
    if h0 is not None:
        h0_re, h0_im = h0
        x_re = x_re.at[:, 0].add(a_re * h0_re - a_im * h0_im)
        x_im = x_im.at[:, 0].add(a_re * h0_im + a_im * h0_re)
    seq_len = u.shape[1]
    ar = jnp.broadcast_to(a_re, (1, seq_len) + a_re.shape)
    ai = jnp.broadcast_to(a_im, (1, seq_len) + a_im.shape)
    _, _, h_re, h_im = lax.associative_scan(_complex_scan_op, (ar, ai, x_re, x_im), axis=1)
    return h_re, h_im


def s5_readout(h_re, h_im, c_re, c_im):
    return jnp.einsum('blgp,gcp->blgc', h_re, c_re) - jnp.einsum('blgp,gcp->blgc', h_im, c_im)


def _orient(t, d):
    return jnp.flip(t, axis=1) if d == 1 else t


def s5_mixer(u_lat, u_ctx, lam_re, lam_im, log_dt, b_re, b_im, c_re, c_im, d_skip, w_glu, ctx_out):
    dtype = u_lat.dtype
    bsz, seq_len, _ = u_lat.shape
    ctx_len = u_ctx.shape[1]
    ul = u_lat.astype(jnp.float32)
    uc = u_ctx.astype(jnp.float32)
    ul_g = ul.reshape(bsz, seq_len, SSM_GROUPS, SSM_GROUP)
    uc_g = uc.reshape(bsz, ctx_len, SSM_GROUPS, SSM_GROUP)
    d32 = d_skip.astype(jnp.float32)
    y_lat = d32 * ul
    y_ctx = d32 * uc if ctx_out else None
    for d in range(N_DIR):
        a_re, a_im, bb_re, bb_im = s5_discretize(lam_re[d], lam_im[d], log_dt[d], b_re[d], b_im[d])
        cr = c_re[d].astype(jnp.float32)
        ci = c_im[d].astype(jnp.float32)
        hc_re, hc_im = s5_states(_orient(uc_g, d), a_re, a_im, bb_re, bb_im)
        hl_re, hl_im = s5_states(_orient(ul_g, d), a_re, a_im, bb_re, bb_im,
                                 h0=(hc_re[:, -1], hc_im[:, -1]))
        y_lat = y_lat + _orient(s5_readout(hl_re, hl_im, cr, ci), d).reshape(bsz, seq_len, SSM_WIDTH)
        if ctx_out:
            y_ctx = y_ctx + _orient(s5_readout(hc_re, hc_im, cr, ci), d).reshape(bsz, ctx_len, SSM_WIDTH)
    w32 = w_glu.astype(jnp.float32)

    def glu(y):
        z = jax.nn.gelu(y)
        return (z * jax.nn.sigmoid(z @ w32)).astype(dtype)

    return glu(y_lat), (glu(y_ctx) if ctx_out else None)


def na_latent(q, k, v, k_ctx, v_ctx, rpb):
    bsz, seq_len, n_heads, head_dim = q.shape
    rows = seq_len // GRID_W
    wr = min(NA_WIN_R, rows)
    r = jnp.arange(rows)
    key_rows = jnp.clip(r - wr // 2, 0, rows - wr)[:, None] + jnp.arange(wr)[None, :]
    col = jnp.arange(GRID_W)
    col_start = jnp.clip(col - NA_WIN_C // 2, 0, GRID_W - NA_WIN_C)
    col_ok = (col[None, :] >= col_start[:, None]) & (col[None, :] < col_start[:, None] + NA_WIN_C)
    row_idx = (key_rows - r[:, None]) + (NA_WIN_R - 1)
    col_idx = jnp.clip(col[None, :] - col[:, None] + (NA_WIN_C - 1), 0, 2 * NA_WIN_C - 2)
    bias = rpb.astype(jnp.float32)[:, row_idx[:, None, :, None], col_idx[None, :, None, :]]

    qg = q.reshape(bsz, rows, GRID_W, n_heads, head_dim)
    kg = k.reshape(bsz, rows, GRID_W, n_heads, head_dim)[:, key_rows]
    vg = v.reshape(bsz, rows, GRID_W, n_heads, head_dim)[:, key_rows]
    scale = head_dim ** -0.5
    s_loc = jnp.einsum('brqhd,brikhd->bhrqik', qg, kg, preferred_element_type=jnp.float32) * scale + bias[None]
    s_loc = jnp.where(col_ok[:, None, :], s_loc, jnp.finfo(jnp.float32).min)
    n_loc = wr * GRID_W
    s_loc = s_loc.reshape(bsz, n_heads, rows, GRID_W, n_loc)
    s_ctx = jnp.einsum('brqhd,bchd->bhrqc', qg, k_ctx, preferred_element_type=jnp.float32) * scale
    p = jax.nn.softmax(jnp.concatenate([s_loc, s_ctx], axis=-1), axis=-1).astype(v.dtype)
    p_loc = p[..., :n_loc].reshape(bsz, n_heads, rows, GRID_W, wr, GRID_W)
    p_ctx = p[..., n_loc:]
    o = (jnp.einsum('bhrqik,brikhd->brqhd', p_loc, vg)
         + jnp.einsum('bhrqc,bchd->brqhd', p_ctx, v_ctx))
    return o.reshape(bsz, seq_len, n_heads * head_dim)


def ctx_attention(q, k, v):
    bsz, ctx_len, n_heads, head_dim = q.shape
    s = jnp.einsum('bqhd,bkhd->bhqk', q, k, preferred_element_type=jnp.float32) * head_dim ** -0.5
    p = jax.nn.softmax(s, axis=-1).astype(v.dtype)
    return jnp.einsum('bhqk,bkhd->bqhd', p, v).reshape(bsz, ctx_len, n_heads * head_dim)


def _heads(t):
    return t.reshape(t.shape[0], t.shape[1], NA_HEADS, NA_HEAD_DIM)


def hybrid_mixer(h, hc, w_in, lam_re, lam_im, log_dt, b_re, b_im, c_re, c_im, d_skip, w_glu, rpb, w_out, ctx_out):
    splits = [SSM_WIDTH, SSM_WIDTH + NA_WIDTH, SSM_WIDTH + 2 * NA_WIDTH]
    u, q, k, v = jnp.split(h @ w_in, splits, axis=-1)
    if ctx_out:
        uc, qc, kc, vc = jnp.split(hc @ w_in, splits, axis=-1)
    else:
        uc = hc @ w_in[:, :SSM_WIDTH]
        kc, vc = jnp.split(hc @ w_in[:, SSM_WIDTH + NA_WIDTH:], 2, axis=-1)
    y_s5, y_s5_c = s5_mixer(u, uc, lam_re, lam_im, log_dt, b_re, b_im, c_re, c_im, d_skip, w_glu, ctx_out)
    y_na = na_latent(_heads(q), _heads(k), _heads(v), _heads(kc), _heads(vc), rpb)
    y = jnp.concatenate([y_s5, y_na], axis=-1) @ w_out
    y_c = None
    if ctx_out:
        y_na_c = ctx_attention(_heads(qc), _heads(kc), _heads(vc))
        y_c = jnp.concatenate([y_s5_c, y_na_c], axis=-1) @ w_out
    return y, y_c


def conformer_conv(h, w_pw1, dw_w, dw_b, ln_g, ln_b, w_pw2):
    a, g = jnp.split(h @ w_pw1, 2, axis=-1)
    z = dwconv1d(a * jax.nn.sigmoid(g), dw_w, dw_b)
    z = jax.nn.silu(layer_norm(z, ln_g, ln_b))
    return z @ w_pw2


def conv_ffn(h, w_up, conv_w, conv_b, w_down):
    u, g = jnp.split(dwconv1d(h @ w_up, conv_w, conv_b), 2, axis=-1)
    return (jax.nn.silu(g) * u) @ w_down


def setup_inputs(seed: int = 0) -> dict:
    key = jax.random.key(seed)
    ks = iter(jax.random.split(key, 48))

    def nrm(shape, std):
        return jax.random.normal(next(ks), shape, jnp.float32) * std

    D = D_MODEL
    G, P, Cg = SSM_GROUPS, SSM_STATE, SSM_GROUP
    n = jnp.arange(P, dtype=jnp.float32)
    return {
        'x': nrm((BATCH, SEQ, D), 1.0),
        'c': nrm((BATCH, D), 1.0),
        'ctx': nrm((BATCH, CTX_LEN, D), 1.0),
        'c_ctx': nrm((D,), 1.0),
        'w_mod': nrm((DEPTH, D, N_MOD * D), 0.5 * D ** -0.5),
        'b_mod': nrm((DEPTH, N_MOD * D), 0.02),
        'g_mix': 1.0 + nrm((DEPTH, D), 0.02),
        'g_ffn': 1.0 + nrm((DEPTH, D), 0.02),
        'w_in': nrm((N_EVEN, D, IN_WIDTH), D ** -0.5),
        'ssm_lam_re': -0.5 + nrm((N_EVEN, N_DIR, G, P), 0.01),
        'ssm_lam_im': math.pi * n + nrm((N_EVEN, N_DIR, G, P), 0.01),
        'ssm_log_dt': jax.random.uniform(next(ks), (N_EVEN, N_DIR, G), jnp.float32,
                                         math.log(DT_MIN), math.log(DT_MAX)),
        'ssm_b_re': nrm((N_EVEN, N_DIR, G, P, Cg), (2 * Cg) ** -0.5),
        'ssm_b_im': nrm((N_EVEN, N_DIR, G, P, Cg), (2 * Cg) ** -0.5),
        'ssm_c_re': nrm((N_EVEN, N_DIR, G, Cg, P), 0.5),
        'ssm_c_im': nrm((N_EVEN, N_DIR, G, Cg, P), 0.5),
        'ssm_d': nrm((N_EVEN, SSM_WIDTH), 1.0),
        'ssm_w_glu': nrm((N_EVEN, SSM_WIDTH, SSM_WIDTH), SSM_WIDTH ** -0.5),
        'na_rpb': nrm((N_EVEN, NA_HEADS, 2 * NA_WIN_R - 1, 2 * NA_WIN_C - 1), 0.1),
        'w_out': nrm((N_EVEN, MIX_OUT, D), MIX_OUT ** -0.5),
        'cv_w_pw1': nrm((N_ODD, D, 2 * CONV_WIDTH), D ** -0.5),
        'cv_dw_w': nrm((N_ODD, CONV_K, CONV_WIDTH), CONV_K ** -0.5),
        'cv_dw_b': nrm((N_ODD, CONV_WIDTH), 0.02),
        'cv_ln_g': 1.0 + nrm((N_ODD, CONV_WIDTH), 0.02),
        'cv_ln_b': nrm((N_ODD, CONV_WIDTH), 0.02),
        'cv_w_pw2': nrm((N_ODD, CONV_WIDTH, D), CONV_WIDTH ** -0.5),
        'ffn_w_up': nrm((DEPTH, D, 2 * FFN_DIM), D ** -0.5),
        'ffn_conv_w': nrm((DEPTH, FFN_CONV_K, 2 * FFN_DIM), FFN_CONV_K ** -0.5),
        'ffn_conv_b': nrm((DEPTH, 2 * FFN_DIM), 0.02),
        'ffn_w_down': nrm((DEPTH, FFN_DIM, D), FFN_DIM ** -0.5),
        'g_out': 1.0 + nrm((D,), 0.02),
    }


def reference(x, c, ctx, c_ctx, w_mod, b_mod, g_mix, g_ffn, w_in, ssm_lam_re, ssm_lam_im, ssm_log_dt,
              ssm_b_re, ssm_b_im, ssm_c_re, ssm_c_im, ssm_d, ssm_w_glu, na_rpb, w_out,
              cv_w_pw1, cv_dw_w, cv_dw_b, cv_ln_g, cv_ln_b, cv_w_pw2,
              ffn_w_up, ffn_conv_w, ffn_conv_b, ffn_w_down, g_out):
    x_ctx = ctx
    s_lat = jax.nn.silu(c)[:, None, :]
    s_ctx = jax.nn.silu(c_ctx)[None, None, :]
    for i in range(DEPTH):
        reads_ctx = (i % 2 == 0)
        ctx_next = any(j % 2 == 0 for j in range(i + 1, DEPTH))
        mod = jnp.split(s_lat @ w_mod[i] + b_mod[i], N_MOD, axis=-1)
        h = modulate(rms_norm(x, g_mix[i]), mod[0], mod[1])
        if reads_ctx or ctx_next:
            mod_c = jnp.split(s_ctx @ w_mod[i] + b_mod[i], N_MOD, axis=-1)
            hc = modulate(rms_norm(x_ctx, g_mix[i]), mod_c[0], mod_c[1])
        if reads_ctx:
            e = i // 2
            y, yc = hybrid_mixer(h, hc, w_in[e], ssm_lam_re[e], ssm_lam_im[e], ssm_log_dt[e],
                                 ssm_b_re[e], ssm_b_im[e], ssm_c_re[e], ssm_c_im[e], ssm_d[e],
                                 ssm_w_glu[e], na_rpb[e], w_out[e], ctx_next)
        else:
            o = i // 2
            y = conformer_conv(h, cv_w_pw1[o], cv_dw_w[o], cv_dw_b[o], cv_ln_g[o], cv_ln_b[o], cv_w_pw2[o])
            yc = (conformer_conv(hc, cv_w_pw1[o], cv_dw_w[o], cv_dw_b[o], cv_ln_g[o], cv_ln_b[o], cv_w_pw2[o])
                  if ctx_next else None)
        x = x + mod[2] * y
        h = modulate(rms_norm(x, g_ffn[i]), mod[3], mod[4])
        x = x + mod[5] * conv_ffn(h, ffn_w_up[i], ffn_conv_w[i], ffn_conv_b[i], ffn_w_down[i])
        if ctx_next:
            x_ctx = x_ctx + mod_c[2] * yc
            hc = modulate(rms_norm(x_ctx, g_ffn[i]), mod_c[3], mod_c[4])
            x_ctx = x_ctx + mod_c[5] * conv_ffn(hc, ffn_w_up[i], ffn_conv_w[i], ffn_conv_b[i], ffn_w_down[i])
    return rms_norm(x, g_out)
```

```python
import functools
import math

import jax
import jax.numpy as jnp
from jax import lax
from jax.experimental import pallas as pl
from jax.experimental.pallas import tpu as pltpu

F32 = jnp.float32
BF16 = jnp.bfloat16

EPS = 1e-6
N_MOD = 6
GRID_W = 64
SSM_GROUP = 16
SSM_STATE = 64
S5_CHUNK = 16
NA_HEAD_DIM = 128
NA_WIN_R = 8
NA_WIN_C = 16
NA_QROWS = 8
NA_KROWS = 16
CONV_K = 31
MASK_VALUE = -1e30

VMEM_LIMIT = 56 * 1024 * 1024


def _params(*sem):
    return pltpu.CompilerParams(dimension_semantics=sem, vmem_limit_bytes=VMEM_LIMIT)


def _rms_mod(x, g, shift, scale):
    ms = jnp.mean(x * x, axis=-1, keepdims=True)
    y = x * lax.rsqrt(ms + EPS) * g
    return y * (1.0 + scale) + shift


def _silu(x):
    return x * jax.nn.sigmoid(x)


def _mod_kernel(c_ref, w_ref, b_ref, o_ref):
    s = _silu(c_ref[...]).astype(BF16)
    o_ref[0] = jnp.dot(s, w_ref[0].astype(BF16), preferred_element_type=F32) + b_ref[0]


def _modulation(c8, w_mod, b_mod, tn=1536):
    depth, d, n = w_mod.shape
    return pl.pallas_call(
        _mod_kernel,
        grid=(depth, n // tn),
        in_specs=[
            pl.BlockSpec((8, d), lambda l, j: (0, 0)),
            pl.BlockSpec((1, d, tn), lambda l, j: (l, 0, j)),
            pl.BlockSpec((1, 1, tn), lambda l, j: (l, 0, j)),
        ],
        out_specs=pl.BlockSpec((1, 8, tn), lambda l, j: (l, 0, j)),
        out_shape=jax.ShapeDtypeStruct((depth, 8, n), F32),
        compiler_params=_params("arbitrary", "arbitrary"),
        name="modulation",
    )(c8, w_mod, b_mod.reshape(depth, 1, n))


def _in_proj_kernel(x_ref, g_ref, shift_ref, scale_ref, w_ref, u_ref, q_ref, k_ref, v_ref, h_scr,
                    *, q_scale):
    j = pl.program_id(1)

    @pl.when(j == 0)
    def _():
        h_scr[...] = _rms_mod(x_ref[...], g_ref[...], shift_ref[0], scale_ref[0]).astype(BF16)

    acc = jnp.dot(h_scr[...], w_ref[...], preferred_element_type=F32)

    @pl.when(j == 0)
    def _():
        u_ref[...] = acc

    @pl.when(j == 1)
    def _():
        q_ref[...] = (acc * q_scale).astype(BF16)

    @pl.when(j == 2)
    def _():
        k_ref[...] = acc.astype(BF16)

    @pl.when(j == 3)
    def _():
        v_ref[...] = acc.astype(BF16)


def _in_proj(x2d, g, mod, mod_row, w_bf16, tm):
    n, d = x2d.shape
    tn = w_bf16.shape[1] // 4
    out = lambda dt: jax.ShapeDtypeStruct((n, tn), dt)
    return pl.pallas_call(
        functools.partial(_in_proj_kernel, q_scale=NA_HEAD_DIM ** -0.5),
        grid=(n // tm, 4),
        in_specs=[
            pl.BlockSpec((tm, d), lambda i, j: (i, 0)),
            pl.BlockSpec((1, d), lambda i, j: (0, 0)),
            pl.BlockSpec((1, 1, d), lambda i, j: (mod_row(i) + 0, 0, 0)),
            pl.BlockSpec((1, 1, d), lambda i, j: (mod_row(i) + 1, 0, 0)),
            pl.BlockSpec((d, tn), lambda i, j: (0, j)),
        ],
        out_specs=[pl.BlockSpec((tm, tn), lambda i, j: (i, 0))] * 4,
        out_shape=[out(F32), out(BF16), out(BF16), out(BF16)],
        scratch_shapes=[pltpu.VMEM((tm, d), BF16)],
        compiler_params=_params("arbitrary", "arbitrary"),
        name="in_proj",
    )(x2d, g.reshape(1, d), mod, mod, w_bf16)


def _s5_matrices(lam_re, lam_im, log_dt, b_re, b_im, c_re, c_im):
    hi = lax.Precision.HIGHEST
    t_len = S5_CHUNK
    lam_re, lam_im, log_dt = lam_re.astype(F32), lam_im.astype(F32), log_dt.astype(F32)
    b_re, b_im, c_re, c_im = (t.astype(F32) for t in (b_re, b_im, c_re, c_im))
    dt = jnp.exp(log_dt)[..., None]
    mag = jnp.exp(lam_re * dt)
    a_re = mag * jnp.cos(lam_im * dt)
    a_im = mag * jnp.sin(lam_im * dt)
    den = jnp.square(lam_re) + jnp.square(lam_im)
    f_re = ((a_re - 1.0) * lam_re + a_im * lam_im) / den
    f_im = (a_im * lam_re - (a_re - 1.0) * lam_im) / den
    bb_re = f_re[..., None] * b_re - f_im[..., None] * b_im
    bb_im = f_re[..., None] * b_im + f_im[..., None] * b_re

    pw_re, pw_im = [jnp.ones_like(a_re)], [jnp.zeros_like(a_re)]
    for _ in range(t_len):
        pr, pi = pw_re[-1], pw_im[-1]
        pw_re.append(pr * a_re - pi * a_im)
        pw_im.append(pr * a_im + pi * a_re)
    pw_re, pw_im = jnp.stack(pw_re), jnp.stack(pw_im)

    e_re = c_re[None] * pw_re[:, :, :, None, :] - c_im[None] * pw_im[:, :, :, None, :]
    e_im = c_re[None] * pw_im[:, :, :, None, :] + c_im[None] * pw_re[:, :, :, None, :]
    kern = (jnp.einsum('tdgop,dgpi->tdgoi', e_re[:t_len], bb_re, precision=hi)
            - jnp.einsum('tdgop,dgpi->tdgoi', e_im[:t_len], bb_im, precision=hi))

    step = jnp.arange(t_len)
    mts, wss_re, wss_im, wos_re, wos_im = [], [], [], [], []
    for d in range(2):
        lag = (step[None, :] - step[:, None]) * (1 if d == 0 else -1)
        kd = jnp.where((lag >= 0)[:, :, None, None, None], kern[:, d][jnp.clip(lag, 0, t_len - 1)], 0.0)
        n_grp = kd.shape[2]
        mts.append(kd.transpose(2, 0, 4, 1, 3).reshape(n_grp, t_len * SSM_GROUP, t_len * SSM_GROUP))
        ex_in = (t_len - 1 - step) if d == 0 else step
        pr, pi = pw_re[ex_in, d][..., None], pw_im[ex_in, d][..., None]
        ws_re = pr * bb_re[d][None] - pi * bb_im[d][None]
        ws_im = pr * bb_im[d][None] + pi * bb_re[d][None]
        to_ws = lambda w: w.transpose(1, 0, 3, 2).reshape(n_grp, t_len * SSM_GROUP, SSM_STATE)
        wss_re.append(to_ws(ws_re))
        wss_im.append(to_ws(ws_im))
        ex_out = (step + 1) if d == 0 else (t_len - step)
        to_wo = lambda w: w.transpose(1, 3, 0, 2).reshape(n_grp, SSM_STATE, t_len * SSM_GROUP)
        wos_re.append(to_wo(e_re[ex_out, d]))
        wos_im.append(to_wo(-e_im[ex_out, d]))
    stack = lambda xs: jnp.stack(xs).astype(BF16)
    a_t_re = pw_re[t_len][:, :, None, :]
    a_t_im = pw_im[t_len][:, :, None, :]
    return stack(mts), stack(wss_re), stack(wss_im), stack(wos_re), stack(wos_im), a_t_re, a_t_im


def _s5_kernel(ul_ref, uc_ref, mt_ref, wsr_ref, wsi_ref, wor_ref, woi_ref, ar_ref, ai_ref, y_ref,
               slr, sli, scr, sci, hr, hi, *, bsz, n_lat, n_ctx):
    ul = ul_ref[0]
    uc = uc_ref[0]
    for d in range(2):
        slr[d] = jnp.dot(ul, wsr_ref[d, 0], preferred_element_type=F32)
        sli[d] = jnp.dot(ul, wsi_ref[d, 0], preferred_element_type=F32)
        scr[d] = jnp.dot(uc, wsr_ref[d, 0], preferred_element_type=F32)
        sci[d] = jnp.dot(uc, wsi_ref[d, 0], preferred_element_type=F32)

    a_re = [ar_ref[d, 0] for d in range(2)]
    a_im = [ai_ref[d, 0] for d in range(2)]
    chains = [(d, b) for d in range(2) for b in range(bsz)]

    def advance(d, h_re, h_im, s_re, s_im):
        return (a_re[d] * h_re - a_im[d] * h_im + s_re, a_re[d] * h_im + a_im[d] * h_re + s_im)

    def ctx_step(i, carry):
        out = []
        for (d, b), (h_re, h_im) in zip(chains, carry):
            row = b * n_ctx + (i if d == 0 else n_ctx - 1 - i)
            out.append(advance(d, h_re, h_im, scr[d, pl.ds(row, 1), :], sci[d, pl.ds(row, 1), :]))
        return tuple(out)

    def lat_step(i, carry):
        out = []
        for (d, b), (h_re, h_im) in zip(chains, carry):
            row = b * n_lat + (i if d == 0 else n_lat - 1 - i)
            hr[d, pl.ds(row, 1), :] = h_re
            hi[d, pl.ds(row, 1), :] = h_im
            out.append(advance(d, h_re, h_im, slr[d, pl.ds(row, 1), :], sli[d, pl.ds(row, 1), :]))
        return tuple(out)

    zero = jnp.zeros((1, SSM_STATE), F32)
    carry = lax.fori_loop(0, n_ctx, ctx_step, tuple((zero, zero) for _ in chains))
    lax.fori_loop(0, n_lat, lat_step, carry)

    y = None
    for d in range(2):
        yd = (jnp.dot(ul, mt_ref[d, 0], preferred_element_type=F32)
              + jnp.dot(hr[d].astype(BF16), wor_ref[d, 0], preferred_element_type=F32)
              + jnp.dot(hi[d].astype(BF16), woi_ref[d, 0], preferred_element_type=F32))
        y = yd if y is None else y + yd
    y_ref[0] = y


def _s5_scan(ul, uc, mats, bsz):
    n_grp, rows_l, width = ul.shape
    rows_c = uc.shape[1]
    mt, ws_re, ws_im, wo_re, wo_im, a_re, a_im = mats
    per_g = lambda r, c: pl.BlockSpec((2, 1, r, c), lambda g: (0, g, 0, 0))
    return pl.pallas_call(
        functools.partial(_s5_kernel, bsz=bsz, n_lat=rows_l // bsz, n_ctx=rows_c // bsz),
        grid=(n_grp,),
        in_specs=[
            pl.BlockSpec((1, rows_l, width), lambda g: (g, 0, 0)),
            pl.BlockSpec((1, rows_c, width), lambda g: (g, 0, 0)),
            per_g(width, width), per_g(width, SSM_STATE), per_g(width, SSM_STATE),
            per_g(SSM_STATE, width), per_g(SSM_STATE, width),
            per_g(1, SSM_STATE), per_g(1, SSM_STATE),
        ],
        out_specs=pl.BlockSpec((1, rows_l, width), lambda g: (g, 0, 0)),
        out_shape=jax.ShapeDtypeStruct((n_grp, rows_l, width), F32),
        scratch_shapes=[
            pltpu.VMEM((2, rows_l, SSM_STATE), F32), pltpu.VMEM((2, rows_l, SSM_STATE), F32),
            pltpu.VMEM((2, rows_c, SSM_STATE), F32), pltpu.VMEM((2, rows_c, SSM_STATE), F32),
            pltpu.VMEM((2, rows_l, SSM_STATE), F32), pltpu.VMEM((2, rows_l, SSM_STATE), F32),
        ],
        compiler_params=_params("arbitrary"),
        name="s5_scan",
    )(ul, uc, mt, ws_re, ws_im, wo_re, wo_im, a_re, a_im)


def _to_s5_layout(u2d, bsz):
    n, width = u2d.shape
    n_grp = width // SSM_GROUP
    t = u2d.astype(BF16).reshape(n // S5_CHUNK, S5_CHUNK, n_grp, SSM_GROUP)
    return t.transpose(2, 0, 1, 3).reshape(n_grp, n // S5_CHUNK, S5_CHUNK * SSM_GROUP)


def _from_s5_layout(y):
    n_grp, rows, _ = y.shape
    t = y.reshape(n_grp, rows, S5_CHUNK, SSM_GROUP).transpose(1, 2, 0, 3)
    return t.reshape(rows * S5_CHUNK, n_grp * SSM_GROUP)


def _na_bias_mask(rpb, rows):
    n_heads = rpb.shape[0]
    rq, rk = jnp.arange(NA_QROWS), jnp.arange(NA_KROWS)
    col = jnp.arange(GRID_W)
    col_start = jnp.clip(col - NA_WIN_C // 2, 0, GRID_W - NA_WIN_C)
    col_ok = (col[None, :] >= col_start[:, None]) & (col[None, :] < col_start[:, None] + NA_WIN_C)
    col_idx = jnp.clip(col[None, :] - col[:, None] + (NA_WIN_C - 1), 0, 2 * NA_WIN_C - 2)
    out = []
    for r0 in (0, NA_QROWS, rows - NA_QROWS):
        rq_abs = r0 + rq
        rk_abs = r0 - NA_WIN_R // 2 + rk
        k_start = jnp.clip(rq_abs - NA_WIN_R // 2, 0, rows - NA_WIN_R)
        row_ok = (rk_abs[None, :] >= k_start[:, None]) & (rk_abs[None, :] < k_start[:, None] + NA_WIN_R)
        row_idx = jnp.clip(rk_abs[None, :] - rq_abs[:, None] + (NA_WIN_R - 1), 0, 2 * NA_WIN_R - 2)
        bias = rpb.astype(F32)[:, row_idx[:, None, :, None], col_idx[None, :, None, :]]
        ok = row_ok[:, None, :, None] & col_ok[None, :, None, :]
        out.append(jnp.where(ok[None], bias, MASK_VALUE).reshape(
            n_heads, NA_QROWS * GRID_W, NA_KROWS * GRID_W))
    return jnp.stack(out, axis=1)


def _na_kernel(q_ref, k0, k1, k2, k3, v0, v1, v2, v3, kc_ref, vc_ref, bm_ref, o_ref):
    nt = (((1,), (1,)), ((), ()))
    q = q_ref[0]
    k_loc = jnp.concatenate([k0[0], k1[0], k2[0], k3[0]], axis=0)
    v_loc = jnp.concatenate([v0[0], v1[0], v2[0], v3[0]], axis=0)
    s_loc = lax.dot_general(q, k_loc, nt, preferred_element_type=F32) + bm_ref[0, 0]
    s_ctx = lax.dot_general(q, kc_ref[0], nt, preferred_element_type=F32)
    m = jnp.maximum(jnp.max(s_loc, axis=-1, keepdims=True), jnp.max(s_ctx, axis=-1, keepdims=True))
    e_loc = jnp.exp(s_loc - m)
    e_ctx = jnp.exp(s_ctx - m)
    denom = jnp.sum(e_loc, axis=-1, keepdims=True) + jnp.sum(e_ctx, axis=-1, keepdims=True)
    o = (jnp.dot(e_loc.astype(BF16), v_loc, preferred_element_type=F32)
         + jnp.dot(e_ctx.astype(BF16), vc_ref[0], preferred_element_type=F32))
    o_ref[0] = (o / denom).astype(BF16)


def _na_attention(q, k, v, kc, vc, bias_mask):
    bsz, seq_len, width = q.shape
    ctx_len = kc.shape[1]
    n_heads = width // NA_HEAD_DIM
    rows = seq_len // GRID_W
    n_blk = rows // NA_QROWS
    tq = NA_QROWS * GRID_W
    tk = NA_KROWS * GRID_W // 4
    n_kblk = seq_len // tk

    def kv_spec(part):
        return pl.BlockSpec((1, tk, NA_HEAD_DIM),
                            lambda h, b, r: (b, jnp.clip(2 * r - 1 + part, 0, n_kblk - 1), h))

    variant = lambda r: (r > 0).astype(jnp.int32) + (r == n_blk - 1).astype(jnp.int32)
    return pl.pallas_call(
        _na_kernel,
        grid=(n_heads, bsz, n_blk),
        in_specs=[pl.BlockSpec((1, tq, NA_HEAD_DIM), lambda h, b, r: (b, r, h))]
        + [kv_spec(p) for p in range(4)] + [kv_spec(p) for p in range(4)]
        + [pl.BlockSpec((1, ctx_len, NA_HEAD_DIM), lambda h, b, r: (b, 0, h))] * 2
        + [pl.BlockSpec((1, 1, tq, 4 * tk), lambda h, b, r: (h, variant(r), 0, 0))],
        out_specs=pl.BlockSpec((1, tq, NA_HEAD_DIM), lambda h, b, r: (b, r, h)),
        out_shape=jax.ShapeDtypeStruct((bsz, seq_len, width), BF16),
        compiler_params=_params("arbitrary", "arbitrary", "arbitrary"),
        name="na_attention",
    )(q, k, k, k, k, v, v, v, v, kc, vc, bias_mask)


def _out_proj_kernel(x_ref, u_ref, ys_ref, yna_ref, d_ref, wglu_ref, wo_s5_ref, wo_na_ref, gate_ref, o_ref):
    y = d_ref[...] * u_ref[...] + ys_ref[...]
    z = jax.nn.gelu(y)
    glu = z * jax.nn.sigmoid(jnp.dot(z.astype(BF16), wglu_ref[...], preferred_element_type=F32))
    o = (jnp.dot(glu.astype(BF16), wo_s5_ref[...], preferred_element_type=F32)
         + jnp.dot(yna_ref[...], wo_na_ref[...], preferred_element_type=F32))
    o_ref[...] = x_ref[...] + gate_ref[0] * o


def _out_proj(x2d, u, ys, yna, d_skip, w_glu, w_out, mod, mod_row, tm):
    n, d = x2d.shape
    ws = u.shape[1]
    wn = yna.shape[1]
    return pl.pallas_call(
        _out_proj_kernel,
        grid=(n // tm,),
        in_specs=[
            pl.BlockSpec((tm, d), lambda i: (i, 0)),
            pl.BlockSpec((tm, ws), lambda i: (i, 0)),
            pl.BlockSpec((tm, ws), lambda i: (i, 0)),
            pl.BlockSpec((tm, wn), lambda i: (i, 0)),
            pl.BlockSpec((1, ws), lambda i: (0, 0)),
            pl.BlockSpec((ws, ws), lambda i: (0, 0)),
            pl.BlockSpec((ws, d), lambda i: (0, 0)),
            pl.BlockSpec((wn, d), lambda i: (ws // wn, 0)),
            pl.BlockSpec((1, 1, d), lambda i: (mod_row(i) + 2, 0, 0)),
        ],
        out_specs=pl.BlockSpec((tm, d), lambda i: (i, 0)),
        out_shape=jax.ShapeDtypeStruct((n, d), F32),
        compiler_params=_params("arbitrary"),
        name="out_proj",
    )(x2d, u, ys, yna, d_skip.reshape(1, ws).astype(F32), w_glu, w_out, w_out, mod)


def _ffn_kernel(x_ref, xp_ref, xn_ref, g_ref, shift_ref, scale_ref, gate_ref, wu_ref, wg_ref,
                cwu_ref, cwg_ref, cbu_ref, cbg_ref, wd_ref, gout_ref, o_ref, h_scr,
                *, tm, tiles_per_seq, final_norm):
    i = pl.program_id(0)
    f = pl.program_id(1)

    @pl.when(f == 0)
    def _():
        norm = lambda x: _rms_mod(x, g_ref[...], shift_ref[0], scale_ref[0])
        h_scr[0:tm] = norm(x_ref[...]).astype(BF16)
        first = (i % tiles_per_seq) == 0
        last = (i % tiles_per_seq) == tiles_per_seq - 1
        h_prev = jnp.where(first, 0.0, norm(xp_ref[...]))
        h_next = jnp.where(last, 0.0, norm(xn_ref[...]))
        h_scr[tm:tm + 16] = jnp.concatenate([h_prev, h_next], axis=0).astype(BF16)
        o_ref[...] = jnp.zeros_like(o_ref)

    h = h_scr[...]
    row = lax.broadcasted_iota(jnp.int32, (tm, 1), 0)

    def conv_branch(w_ref, cw_ref, cb_ref):
        up = jnp.dot(h, w_ref[...], preferred_element_type=F32)
        mid = up[0:tm]
        before = jnp.where(row == 0, up[tm + 7:tm + 8], pltpu.roll(mid, 1, axis=0))
        after = jnp.where(row == tm - 1, up[tm + 8:tm + 9], pltpu.roll(mid, tm - 1, axis=0))
        cw = cw_ref[...]
        return cw[0:1] * before + cw[1:2] * mid + cw[2:3] * after + cb_ref[...]

    cu = conv_branch(wu_ref, cwu_ref, cbu_ref)
    cg = conv_branch(wg_ref, cwg_ref, cbg_ref)
    act = (_silu(cg) * cu).astype(BF16)
    o_ref[...] += jnp.dot(act, wd_ref[...], preferred_element_type=F32)

    @pl.when(f == pl.num_programs(1) - 1)
    def _():
        x_new = x_ref[...] + gate_ref[0] * o_ref[...]
        if final_norm:
            ms = jnp.mean(x_new * x_new, axis=-1, keepdims=True)
            x_new = x_new * lax.rsqrt(ms + EPS) * gout_ref[...]
        o_ref[...] = x_new


def _conv_ffn(x2d, g, mod, mod_row, w_up, conv_w, conv_b, w_down, g_out, seq_len, tm, tf, final_norm):
    n, d = x2d.shape
    ffn = w_down.shape[0]
    nf = ffn // tf
    halo = 8
    nh = n // halo
    return pl.pallas_call(
        functools.partial(_ffn_kernel, tm=tm, tiles_per_seq=seq_len // tm, final_norm=final_norm),
        grid=(n // tm, nf),
        in_specs=[
            pl.BlockSpec((tm, d), lambda i, f: (i, 0)),
            pl.BlockSpec((halo, d), lambda i, f: (jnp.maximum(i * (tm // halo) - 1, 0), 0)),
            pl.BlockSpec((halo, d), lambda i, f: (jnp.minimum((i + 1) * (tm // halo), nh - 1), 0)),
            pl.BlockSpec((1, d), lambda i, f: (0, 0)),
            pl.BlockSpec((1, 1, d), lambda i, f: (mod_row(i) + 3, 0, 0)),
            pl.BlockSpec((1, 1, d), lambda i, f: (mod_row(i) + 4, 0, 0)),
            pl.BlockSpec((1, 1, d), lambda i, f: (mod_row(i) + 5, 0, 0)),
            pl.BlockSpec((d, tf), lambda i, f: (0, f)),
            pl.BlockSpec((d, tf), lambda i, f: (0, nf + f)),
            pl.BlockSpec((3, tf), lambda i, f: (0, f)),
            pl.BlockSpec((3, tf), lambda i, f: (0, nf + f)),
            pl.BlockSpec((1, tf), lambda i, f: (0, f)),
            pl.BlockSpec((1, tf), lambda i, f: (0, nf + f)),
            pl.BlockSpec((tf, d), lambda i, f: (f, 0)),
            pl.BlockSpec((1, d), lambda i, f: (0, 0)),
        ],
        out_specs=pl.BlockSpec((tm, d), lambda i, f: (i, 0)),
        out_shape=jax.ShapeDtypeStruct((n, d), F32),
        scratch_shapes=[pltpu.VMEM((tm + 16, d), BF16)],
        compiler_params=_params("arbitrary", "arbitrary"),
        name="conv_ffn",
    )(x2d, x2d, x2d, g.reshape(1, d), mod, mod, mod, w_up, w_up, conv_w, conv_w,
      conv_b.reshape(1, -1), conv_b.reshape(1, -1), w_down, g_out.reshape(1, d))


def _pw1_glu_kernel(x_ref, g_ref, shift_ref, scale_ref, wa_ref, wg_ref, z_ref, h_scr):
    @pl.when(pl.program_id(1) == 0)
    def _():
        h_scr[...] = _rms_mod(x_ref[...], g_ref[...], shift_ref[0], scale_ref[0]).astype(BF16)

    h = h_scr[...]
    a = jnp.dot(h, wa_ref[...], preferred_element_type=F32)
    gate = jnp.dot(h, wg_ref[...], preferred_element_type=F32)
    z_ref[...] = a * jax.nn.sigmoid(gate)


def _pw1_glu(x2d, g, mod, mod_row, w_pw1, tm, tn):
    n, d = x2d.shape
    width = w_pw1.shape[1] // 2
    nj = width // tn
    return pl.pallas_call(
        _pw1_glu_kernel,
        grid=(n // tm, nj),
        in_specs=[
            pl.BlockSpec((tm, d), lambda i, j: (i, 0)),
            pl.BlockSpec((1, d), lambda i, j: (0, 0)),
            pl.BlockSpec((1, 1, d), lambda i, j: (mod_row(i) + 0, 0, 0)),
            pl.BlockSpec((1, 1, d), lambda i, j: (mod_row(i) + 1, 0, 0)),
            pl.BlockSpec((d, tn), lambda i, j: (0, j)),
            pl.BlockSpec((d, tn), lambda i, j: (0, nj + j)),
        ],
        out_specs=pl.BlockSpec((tm, tn), lambda i, j: (i, j)),
        out_shape=jax.ShapeDtypeStruct((n, width), F32),
        scratch_shapes=[pltpu.VMEM((tm, d), BF16)],
        compiler_params=_params("arbitrary", "arbitrary"),
        name="pw1_glu",
    )(x2d, g.reshape(1, d), mod, mod, w_pw1, w_pw1)


def _dwconv_ln_pw2_kernel(z_ref, zp_ref, zn_ref, dw_ref, db_ref, lg_ref, lb_ref, w2_ref, x_ref, gate_ref,
                          o_ref, zs_scr, cv_scr, *, tm, tiles_per_seq, row_blk):
    i = pl.program_id(0)
    pad = CONV_K // 2 + 1
    first = (i % tiles_per_seq) == 0
    last = (i % tiles_per_seq) == tiles_per_seq - 1
    zs_scr[0:pad] = jnp.where(first, 0.0, zp_ref[...])
    zs_scr[pad:pad + tm] = z_ref[...]
    zs_scr[pad + tm:2 * pad + tm] = jnp.where(last, 0.0, zn_ref[...])

    lanes = 128

    def conv_cols(c, carry):
        cols = pl.ds(pl.multiple_of(c * lanes, lanes), lanes)
        for rb in range(tm // row_blk):
            acc = jnp.zeros((row_blk, lanes), F32) + db_ref[:, cols]
            for k in range(CONV_K):
                acc = acc + zs_scr[pl.ds(rb * row_blk + 1 + k, row_blk), cols] * dw_ref[k:k + 1, cols]
            cv_scr[rb * row_blk:(rb + 1) * row_blk, cols] = acc
        return carry

    lax.fori_loop(0, cv_scr.shape[1] // lanes, conv_cols, 0)

    v = cv_scr[...]
    mu = jnp.mean(v, axis=-1, keepdims=True)
    var = jnp.mean(jnp.square(v - mu), axis=-1, keepdims=True)
    y = _silu((v - mu) * lax.rsqrt(var + EPS) * lg_ref[...] + lb_ref[...])
    o = jnp.dot(y.astype(BF16), w2_ref[...], preferred_element_type=F32)
    o_ref[...] = x_ref[...] + gate_ref[0] * o


def _dwconv_ln_pw2(z, x2d, dw_w, dw_b, ln_g, ln_b, w_pw2, mod, mod_row, seq_len, tm):
    n, width = z.shape
    d = x2d.shape[1]
    pad = CONV_K // 2 + 1
    nh = n // pad
    row = lambda t: t.reshape(1, -1).astype(F32)
    return pl.pallas_call(
        functools.partial(_dwconv_ln_pw2_kernel, tm=tm, tiles_per_seq=seq_len // tm, row_blk=64),
        grid=(n // tm,),
        in_specs=[
            pl.BlockSpec((tm, width), lambda i: (i, 0)),
            pl.BlockSpec((pad, width), lambda i: (jnp.maximum(i * (tm // pad) - 1, 0), 0)),
            pl.BlockSpec((pad, width), lambda i: (jnp.minimum((i + 1) * (tm // pad), nh - 1), 0)),
            pl.BlockSpec((CONV_K, width), lambda i: (0, 0)),
            pl.BlockSpec((1, width), lambda i: (0, 0)),
            pl.BlockSpec((1, width), lambda i: (0, 0)),
            pl.BlockSpec((1, width), lambda i: (0, 0)),
            pl.BlockSpec((width, d), lambda i: (0, 0)),
            pl.BlockSpec((tm, d), lambda i: (i, 0)),
            pl.BlockSpec((1, 1, d), lambda i: (mod_row(i) + 2, 0, 0)),
        ],
        out_specs=pl.BlockSpec((tm, d), lambda i: (i, 0)),
        out_shape=jax.ShapeDtypeStruct((n, d), F32),
        scratch_shapes=[pltpu.VMEM((tm + 2 * pad, width), F32), pltpu.VMEM((tm, width), F32)],
        compiler_params=_params("arbitrary"),
        name="dwconv_ln_pw2",
    )(z, z, z, dw_w.astype(F32), row(dw_b), row(ln_g), row(ln_b), w_pw2, x2d, mod)


def kernel(x, c, ctx, c_ctx, w_mod, b_mod, g_mix, g_ffn, w_in, ssm_lam_re, ssm_lam_im, ssm_log_dt, ssm_b_re, ssm_b_im, ssm_c_re, ssm_c_im, ssm_d, ssm_w_glu, na_rpb, w_out, cv_w_pw1, cv_dw_w, cv_dw_b, cv_ln_g, cv_ln_b, cv_w_pw2, ffn_w_up, ffn_conv_w, ffn_conv_b, ffn_w_down, g_out):
    bsz, seq_len, d = x.shape
    ctx_len = ctx.shape[1]
    n = bsz * seq_len
    ssm_width = ssm_d.shape[1]
    assert w_mod.shape[0] == 2 and bsz < 8, "two layers: hybrid mixer, then Conformer convolution"

    c8 = jnp.concatenate([c, c_ctx[None], jnp.zeros((8 - bsz - 1, d), c.dtype)], axis=0).astype(F32)
    mod = _modulation(c8, w_mod, b_mod).reshape(2 * 8 * N_MOD, 1, d)
    tm = 512
    lat_row = lambda layer, tile: (lambda i: (layer * 8 + i // (seq_len // tile)) * N_MOD)
    ctx_row = lambda i: bsz * N_MOD

    x2d = x.reshape(n, d)
    ctx2d = ctx.reshape(bsz * ctx_len, d)

    w_in0 = w_in[0].astype(BF16)
    u, q, k, v = _in_proj(x2d, g_mix[0], mod, lat_row(0, tm), w_in0, tm)
    uc, _, kc, vc = _in_proj(ctx2d, g_mix[0], mod, ctx_row, w_in0, ctx_len)

    mats = _s5_matrices(ssm_lam_re[0], ssm_lam_im[0], ssm_log_dt[0], ssm_b_re[0], ssm_b_im[0],
                        ssm_c_re[0], ssm_c_im[0])
    ys = _from_s5_layout(_s5_scan(_to_s5_layout(u, bsz), _to_s5_layout(uc, bsz), mats, bsz))

    to3 = lambda t, length: t.reshape(bsz, length, t.shape[-1])
    yna = _na_attention(to3(q, seq_len), to3(k, seq_len), to3(v, seq_len), to3(kc, ctx_len), to3(vc, ctx_len),
                        _na_bias_mask(na_rpb[0], seq_len // GRID_W))
    x2d = _out_proj(x2d, u, ys, yna.reshape(n, -1), ssm_d[0], ssm_w_glu[0].astype(BF16),
                    w_out[0].astype(BF16), mod, lat_row(0, 256), 256)
    x2d = _conv_ffn(x2d, g_ffn[0], mod, lat_row(0, tm), ffn_w_up[0].astype(BF16), ffn_conv_w[0].astype(F32),
                    ffn_conv_b[0].astype(F32), ffn_w_down[0].astype(BF16), g_out, seq_len, tm, 512, False)

    z = _pw1_glu(x2d, g_mix[1], mod, lat_row(1, tm), cv_w_pw1[0].astype(BF16), tm, min(1024, d))
    x2d = _dwconv_ln_pw2(z, x2d, cv_dw_w[0], cv_dw_b[0], cv_ln_g[0], cv_ln_b[0], cv_w_pw2[0].astype(BF16),
                         mod, lat_row(1, 256), seq_len, 256)
    x2d = _conv_ffn(x2d, g_ffn[1], mod, lat_row(1, tm), ffn_w_up[1].astype(BF16), ffn_conv_w[1].astype(F32),
                    ffn_conv_b[1].astype(F32), ffn_w_down[1].astype(BF16), g_out, seq_len, tm, 512, True)
    return x2d.reshape(bsz, seq_len, d).astype(x.dtype)
```

```python
import functools
import math

import jax
import jax.numpy as jnp
from jax import lax
from jax.experimental import pallas as pl
from jax.experimental.pallas import tpu as pltpu

F32 = jnp.float32
BF16 = jnp.bfloat16

EPS = 1e-6
N_MOD = 6
GRID_W = 64
SSM_GROUP = 16
SSM_STATE = 64
S5_CHUNK = 16
NA_HEAD_DIM = 128
NA_WIN_R = 8
NA_WIN_C = 16
NA_QROWS = 8
NA_KROWS = 16
CONV_K = 31
FFN_SUB = 512
MASK_VALUE = -1e30

VMEM_LIMIT = 56 * 1024 * 1024


def _params(*sem):
    return pltpu.CompilerParams(dimension_semantics=sem, vmem_limit_bytes=VMEM_LIMIT)


def _rms_mod(x, g, shift, scale):
    ms = jnp.mean(x * x, axis=-1, keepdims=True)
    y = x * lax.rsqrt(ms + EPS) * g
    return y * (1.0 + scale) + shift


def _silu(x):
    return x * jax.nn.sigmoid(x)


def _mod_kernel(c_ref, w_ref, b_ref, o_ref):
    s = _silu(c_ref[...]).astype(BF16)
    o_ref[0] = jnp.dot(s, w_ref[0].astype(BF16), preferred_element_type=F32) + b_ref[0]


def _modulation(c8, w_mod, b_mod, tn=1536):
    depth, d, n = w_mod.shape
    return pl.pallas_call(
        _mod_kernel,
        grid=(depth, n // tn),
        in_specs=[
            pl.BlockSpec((8, d), lambda l, j: (0, 0)),
            pl.BlockSpec((1, d, tn), lambda l, j: (l, 0, j)),
            pl.BlockSpec((1, 1, tn), lambda l, j: (l, 0, j)),
        ],
        out_specs=pl.BlockSpec((1, 8, tn), lambda l, j: (l, 0, j)),
        out_shape=jax.ShapeDtypeStruct((depth, 8, n), F32),
        compiler_params=_params("arbitrary", "arbitrary"),
        name="modulation",
    )(c8, w_mod, b_mod.reshape(depth, 1, n))


def _in_proj_kernel(x_ref, g_ref, shift_ref, scale_ref, w_ref, u_ref, q_ref, k_ref, v_ref, h_scr,
                    *, q_scale):
    j = pl.program_id(1)

    @pl.when(j == 0)
    def _():
        h_scr[...] = _rms_mod(x_ref[...], g_ref[...], shift_ref[0], scale_ref[0]).astype(BF16)

    acc = jnp.dot(h_scr[...], w_ref[...], preferred_element_type=F32)

    @pl.when(j == 0)
    def _():
        u_ref[...] = acc

    @pl.when(j == 1)
    def _():
        q_ref[...] = (acc * q_scale).astype(BF16)

    @pl.when(j == 2)
    def _():
        k_ref[...] = acc.astype(BF16)

    @pl.when(j == 3)
    def _():
        v_ref[...] = acc.astype(BF16)


def _in_proj(x2d, g, mod, mod_row, w_bf16, tm):
    n, d = x2d.shape
    tn = w_bf16.shape[1] // 4
    out = lambda dt: jax.ShapeDtypeStruct((n, tn), dt)
    return pl.pallas_call(
        functools.partial(_in_proj_kernel, q_scale=NA_HEAD_DIM ** -0.5),
        grid=(n // tm, 4),
        in_specs=[
            pl.BlockSpec((tm, d), lambda i, j: (i, 0)),
            pl.BlockSpec((1, d), lambda i, j: (0, 0)),
            pl.BlockSpec((1, 1, d), lambda i, j: (mod_row(i) + 0, 0, 0)),
            pl.BlockSpec((1, 1, d), lambda i, j: (mod_row(i) + 1, 0, 0)),
            pl.BlockSpec((d, tn), lambda i, j: (0, j)),
        ],
        out_specs=[pl.BlockSpec((tm, tn), lambda i, j: (i, 0))] * 4,
        out_shape=[out(F32), out(BF16), out(BF16), out(BF16)],
        scratch_shapes=[pltpu.VMEM((tm, d), BF16)],
        compiler_params=_params("arbitrary", "arbitrary"),
        name="in_proj",
    )(x2d, g.reshape(1, d), mod, mod, w_bf16)


def _s5_matrices(lam_re, lam_im, log_dt, b_re, b_im, c_re, c_im):
    hi = lax.Precision.HIGHEST
    t_len = S5_CHUNK
    lam_re, lam_im, log_dt = lam_re.astype(F32), lam_im.astype(F32), log_dt.astype(F32)
    b_re, b_im, c_re, c_im = (t.astype(F32) for t in (b_re, b_im, c_re, c_im))
    dt = jnp.exp(log_dt)[..., None]
    mag = jnp.exp(lam_re * dt)
    a_re = mag * jnp.cos(lam_im * dt)
    a_im = mag * jnp.sin(lam_im * dt)
    den = jnp.square(lam_re) + jnp.square(lam_im)
    f_re = ((a_re - 1.0) * lam_re + a_im * lam_im) / den
    f_im = (a_im * lam_re - (a_re - 1.0) * lam_im) / den
    bb_re = f_re[..., None] * b_re - f_im[..., None] * b_im
    bb_im = f_re[..., None] * b_im + f_im[..., None] * b_re

    pw_re, pw_im = [jnp.ones_like(a_re)], [jnp.zeros_like(a_re)]
    for _ in range(t_len):
        pr, pi = pw_re[-1], pw_im[-1]
        pw_re.append(pr * a_re - pi * a_im)
        pw_im.append(pr * a_im + pi * a_re)
    pw_re, pw_im = jnp.stack(pw_re), jnp.stack(pw_im)

    e_re = c_re[None] * pw_re[:, :, :, None, :] - c_im[None] * pw_im[:, :, :, None, :]
    e_im = c_re[None] * pw_im[:, :, :, None, :] + c_im[None] * pw_re[:, :, :, None, :]
    kern = (jnp.einsum('tdgop,dgpi->tdgoi', e_re[:t_len], bb_re, precision=hi)
            - jnp.einsum('tdgop,dgpi->tdgoi', e_im[:t_len], bb_im, precision=hi))

    step = jnp.arange(t_len)
    mts, wss_re, wss_im, wos_re, wos_im = [], [], [], [], []
    for d in range(2):
        lag = (step[None, :] - step[:, None]) * (1 if d == 0 else -1)
        kd = jnp.where((lag >= 0)[:, :, None, None, None], kern[:, d][jnp.clip(lag, 0, t_len - 1)], 0.0)
        n_grp = kd.shape[2]
        mts.append(kd.transpose(2, 0, 4, 1, 3).reshape(n_grp, t_len * SSM_GROUP, t_len * SSM_GROUP))
        ex_in = (t_len - 1 - step) if d == 0 else step
        pr, pi = pw_re[ex_in, d][..., None], pw_im[ex_in, d][..., None]
        ws_re = pr * bb_re[d][None] - pi * bb_im[d][None]
        ws_im = pr * bb_im[d][None] + pi * bb_re[d][None]
        to_ws = lambda w: w.transpose(1, 0, 3, 2).reshape(n_grp, t_len * SSM_GROUP, SSM_STATE)
        wss_re.append(to_ws(ws_re))
        wss_im.append(to_ws(ws_im))
        ex_out = (step + 1) if d == 0 else (t_len - step)
        to_wo = lambda w: w.transpose(1, 3, 0, 2).reshape(n_grp, SSM_STATE, t_len * SSM_GROUP)
        wos_re.append(to_wo(e_re[ex_out, d]))
        wos_im.append(to_wo(-e_im[ex_out, d]))
    stack = lambda xs: jnp.stack(xs).astype(BF16)
    a_t_re = pw_re[t_len][:, :, None, :]
    a_t_im = pw_im[t_len][:, :, None, :]
    return stack(mts), stack(wss_re), stack(wss_im), stack(wos_re), stack(wos_im), a_t_re, a_t_im


def _s5_kernel(ul_ref, uc_ref, mt_ref, wsr_ref, wsi_ref, wor_ref, woi_ref, ar_ref, ai_ref, y_ref,
               slr, sli, scr, sci, hr, hi, *, bsz, n_lat, n_ctx):
    ul = ul_ref[0]
    uc = uc_ref[0]
    for d in range(2):
        slr[d] = jnp.dot(ul, wsr_ref[d, 0], preferred_element_type=F32)
        sli[d] = jnp.dot(ul, wsi_ref[d, 0], preferred_element_type=F32)
        scr[d] = jnp.dot(uc, wsr_ref[d, 0], preferred_element_type=F32)
        sci[d] = jnp.dot(uc, wsi_ref[d, 0], preferred_element_type=F32)

    a_re = [ar_ref[d, 0] for d in range(2)]
    a_im = [ai_ref[d, 0] for d in range(2)]
    chains = [(d, b) for d in range(2) for b in range(bsz)]

    def advance(d, h_re, h_im, s_re, s_im):
        return (a_re[d] * h_re - a_im[d] * h_im + s_re, a_re[d] * h_im + a_im[d] * h_re + s_im)

    def ctx_step(i, carry):
        out = []
        for (d, b), (h_re, h_im) in zip(chains, carry):
            row = b * n_ctx + (i if d == 0 else n_ctx - 1 - i)
            out.append(advance(d, h_re, h_im, scr[d, pl.ds(row, 1), :], sci[d, pl.ds(row, 1), :]))
        return tuple(out)

    def lat_step(i, carry):
        out = []
        for (d, b), (h_re, h_im) in zip(chains, carry):
            row = b * n_lat + (i if d == 0 else n_lat - 1 - i)
            hr[d, pl.ds(row, 1), :] = h_re
            hi[d, pl.ds(row, 1), :] = h_im
            out.append(advance(d, h_re, h_im, slr[d, pl.ds(row, 1), :], sli[d, pl.ds(row, 1), :]))
        return tuple(out)

    zero = jnp.zeros((1, SSM_STATE), F32)
    carry = lax.fori_loop(0, n_ctx, ctx_step, tuple((zero, zero) for _ in chains))
    lax.fori_loop(0, n_lat, lat_step, carry)

    y = None
    for d in range(2):
        yd = (jnp.dot(ul, mt_ref[d, 0], preferred_element_type=F32)
              + jnp.dot(hr[d].astype(BF16), wor_ref[d, 0], preferred_element_type=F32)
              + jnp.dot(hi[d].astype(BF16), woi_ref[d, 0], preferred_element_type=F32))
        y = yd if y is None else y + yd
    y_ref[0] = y


def _s5_scan(ul, uc, mats, bsz):
    n_grp, rows_l, width = ul.shape
    rows_c = uc.shape[1]
    mt, ws_re, ws_im, wo_re, wo_im, a_re, a_im = mats
    per_g = lambda r, c: pl.BlockSpec((2, 1, r, c), lambda g: (0, g, 0, 0))
    return pl.pallas_call(
        functools.partial(_s5_kernel, bsz=bsz, n_lat=rows_l // bsz, n_ctx=rows_c // bsz),
        grid=(n_grp,),
        in_specs=[
            pl.BlockSpec((1, rows_l, width), lambda g: (g, 0, 0)),
            pl.BlockSpec((1, rows_c, width), lambda g: (g, 0, 0)),
            per_g(width, width), per_g(width, SSM_STATE), per_g(width, SSM_STATE),
            per_g(SSM_STATE, width), per_g(SSM_STATE, width),
            per_g(1, SSM_STATE), per_g(1, SSM_STATE),
        ],
        out_specs=pl.BlockSpec((1, rows_l, width), lambda g: (g, 0, 0)),
        out_shape=jax.ShapeDtypeStruct((n_grp, rows_l, width), F32),
        scratch_shapes=[
            pltpu.VMEM((2, rows_l, SSM_STATE), F32), pltpu.VMEM((2, rows_l, SSM_STATE), F32),
            pltpu.VMEM((2, rows_c, SSM_STATE), F32), pltpu.VMEM((2, rows_c, SSM_STATE), F32),
            pltpu.VMEM((2, rows_l, SSM_STATE), F32), pltpu.VMEM((2, rows_l, SSM_STATE), F32),
        ],
        compiler_params=_params("arbitrary"),
        name="s5_scan",
    )(ul, uc, mt, ws_re, ws_im, wo_re, wo_im, a_re, a_im)


def _to_s5_layout(u2d, bsz):
    n, width = u2d.shape
    n_grp = width // SSM_GROUP
    t = u2d.astype(BF16).reshape(n // S5_CHUNK, S5_CHUNK, n_grp, SSM_GROUP)
    return t.transpose(2, 0, 1, 3).reshape(n_grp, n // S5_CHUNK, S5_CHUNK * SSM_GROUP)


def _from_s5_layout(y):
    n_grp, rows, _ = y.shape
    t = y.reshape(n_grp, rows, S5_CHUNK, SSM_GROUP).transpose(1, 2, 0, 3)
    return t.reshape(rows * S5_CHUNK, n_grp * SSM_GROUP)


def _na_bias_mask(rpb, rows):
    n_heads = rpb.shape[0]
    col = jnp.arange(GRID_W)
    col_start = jnp.clip(col - NA_WIN_C // 2, 0, GRID_W - NA_WIN_C)
    col_ok = (col[None, :] >= col_start[:, None]) & (col[None, :] < col_start[:, None] + NA_WIN_C)
    col_idx = jnp.clip(col[None, :] - col[:, None] + (NA_WIN_C - 1), 0, 2 * NA_WIN_C - 2)
    onehot = (col_idx[:, :, None] == jnp.arange(2 * NA_WIN_C - 1)).astype(F32)
    tiles = jnp.einsum('hde,qke->hdqk', rpb.astype(F32), onehot, precision=lax.Precision.HIGHEST)
    tiles = jnp.where(col_ok[None, None], tiles, MASK_VALUE)
    masked = jnp.full((n_heads, GRID_W, GRID_W), MASK_VALUE, F32)
    out = []
    for r0 in (0, NA_QROWS, rows - NA_QROWS):
        blocks = []
        for rq in range(NA_QROWS):
            rq_abs = r0 + rq
            k_start = min(max(rq_abs - NA_WIN_R // 2, 0), rows - NA_WIN_R)
            strip = []
            for rk in range(NA_KROWS):
                rk_abs = r0 - NA_WIN_R // 2 + rk
                inside = k_start <= rk_abs < k_start + NA_WIN_R
                strip.append(tiles[:, rk_abs - rq_abs + NA_WIN_R - 1] if inside else masked)
            blocks.append(jnp.concatenate(strip, axis=-1))
        out.append(jnp.concatenate(blocks, axis=1))
    return jnp.stack(out, axis=1)


def _na_kernel(q_ref, k0, k1, k2, k3, v0, v1, v2, v3, kc_ref, vc_ref, bm_ref, o_ref):
    nt = (((1,), (1,)), ((), ()))
    q = q_ref[0]
    k_loc = jnp.concatenate([k0[0], k1[0], k2[0], k3[0]], axis=0)
    v_loc = jnp.concatenate([v0[0], v1[0], v2[0], v3[0]], axis=0)
    s_loc = lax.dot_general(q, k_loc, nt, preferred_element_type=F32) + bm_ref[0, 0]
    s_ctx = lax.dot_general(q, kc_ref[0], nt, preferred_element_type=F32)
    m = jnp.maximum(jnp.max(s_loc, axis=-1, keepdims=True), jnp.max(s_ctx, axis=-1, keepdims=True))
    e_loc = jnp.exp(s_loc - m)
    e_ctx = jnp.exp(s_ctx - m)
    denom = jnp.sum(e_loc, axis=-1, keepdims=True) + jnp.sum(e_ctx, axis=-1, keepdims=True)
    o = (jnp.dot(e_loc.astype(BF16), v_loc, preferred_element_type=F32)
         + jnp.dot(e_ctx.astype(BF16), vc_ref[0], preferred_element_type=F32))
    o_ref[0] = (o / denom).astype(BF16)


def _na_attention(q, k, v, kc, vc, bias_mask):
    bsz, seq_len, width = q.shape
    ctx_len = kc.shape[1]
    n_heads = width // NA_HEAD_DIM
    rows = seq_len // GRID_W
    n_blk = rows // NA_QROWS
    tq = NA_QROWS * GRID_W
    tk = NA_KROWS * GRID_W // 4
    n_kblk = seq_len // tk

    def kv_spec(part):
        return pl.BlockSpec((1, tk, NA_HEAD_DIM),
                            lambda h, b, r: (b, jnp.clip(2 * r - 1 + part, 0, n_kblk - 1), h))

    variant = lambda r: (r > 0).astype(jnp.int32) + (r == n_blk - 1).astype(jnp.int32)
    return pl.pallas_call(
        _na_kernel,
        grid=(n_heads, bsz, n_blk),
        in_specs=[pl.BlockSpec((1, tq, NA_HEAD_DIM), lambda h, b, r: (b, r, h))]
        + [kv_spec(p) for p in range(4)] + [kv_spec(p) for p in range(4)]
        + [pl.BlockSpec((1, ctx_len, NA_HEAD_DIM), lambda h, b, r: (b, 0, h))] * 2
        + [pl.BlockSpec((1, 1, tq, 4 * tk), lambda h, b, r: (h, variant(r), 0, 0))],
        out_specs=pl.BlockSpec((1, tq, NA_HEAD_DIM), lambda h, b, r: (b, r, h)),
        out_shape=jax.ShapeDtypeStruct((bsz, seq_len, width), BF16),
        compiler_params=_params("arbitrary", "arbitrary", "arbitrary"),
        name="na_attention",
    )(q, k, k, k, k, v, v, v, v, kc, vc, bias_mask)


def _out_proj_kernel(x_ref, u_ref, ys_ref, yna_ref, d_ref, wglu_ref, wo_s5_ref, wo_na_ref, gate_ref, o_ref):
    y = d_ref[...] * u_ref[...] + ys_ref[...]
    z = jax.nn.gelu(y)
    glu = z * jax.nn.sigmoid(jnp.dot(z.astype(BF16), wglu_ref[...], preferred_element_type=F32))
    o = (jnp.dot(glu.astype(BF16), wo_s5_ref[...], preferred_element_type=F32)
         + jnp.dot(yna_ref[...], wo_na_ref[...], preferred_element_type=F32))
    o_ref[...] = x_ref[...] + gate_ref[0] * o


def _out_proj(x2d, u, ys, yna, d_skip, w_glu, w_out, mod, mod_row, tm):
    n, d = x2d.shape
    ws = u.shape[1]
    wn = yna.shape[1]
    return pl.pallas_call(
        _out_proj_kernel,
        grid=(n // tm,),
        in_specs=[
            pl.BlockSpec((tm, d), lambda i: (i, 0)),
            pl.BlockSpec((tm, ws), lambda i: (i, 0)),
            pl.BlockSpec((tm, ws), lambda i: (i, 0)),
            pl.BlockSpec((tm, wn), lambda i: (i, 0)),
            pl.BlockSpec((1, ws), lambda i: (0, 0)),
            pl.BlockSpec((ws, ws), lambda i: (0, 0)),
            pl.BlockSpec((ws, d), lambda i: (0, 0)),
            pl.BlockSpec((wn, d), lambda i: (ws // wn, 0)),
            pl.BlockSpec((1, 1, d), lambda i: (mod_row(i) + 2, 0, 0)),
        ],
        out_specs=pl.BlockSpec((tm, d), lambda i: (i, 0)),
        out_shape=jax.ShapeDtypeStruct((n, d), F32),
        compiler_params=_params("arbitrary"),
        name="out_proj",
    )(x2d, u, ys, yna, d_skip.reshape(1, ws).astype(F32), w_glu, w_out, w_out, mod)


def _ffn_kernel(x_ref, xp_ref, xn_ref, g_ref, shift_ref, scale_ref, gate_ref, wu_ref, wg_ref,
                cwu_ref, cwg_ref, cbu_ref, cbg_ref, wd_ref, gout_ref, o_ref, h_scr,
                *, tm, sub, tiles_per_seq, final_norm):
    i = pl.program_id(0)
    f = pl.program_id(1)

    @pl.when(f == 0)
    def _():
        norm = lambda x: _rms_mod(x, g_ref[...], shift_ref[0], scale_ref[0])
        h_scr[0:tm] = norm(x_ref[...]).astype(BF16)
        first = (i % tiles_per_seq) == 0
        last = (i % tiles_per_seq) == tiles_per_seq - 1
        h_prev = jnp.where(first, 0.0, norm(xp_ref[...]))
        h_next = jnp.where(last, 0.0, norm(xn_ref[...]))
        h_scr[tm:tm + 16] = jnp.concatenate([h_prev, h_next], axis=0).astype(BF16)
        o_ref[...] = jnp.zeros_like(o_ref)

    h = h_scr[...]
    row = lax.broadcasted_iota(jnp.int32, (tm, 1), 0)

    def conv3(up, cw_ref, cb_ref, cols):
        mid = up[0:tm]
        before = jnp.where(row == 0, up[tm + 7:tm + 8], pltpu.roll(mid, 1, axis=0))
        after = jnp.where(row == tm - 1, up[tm + 8:tm + 9], pltpu.roll(mid, tm - 1, axis=0))
        cw = cw_ref[:, cols]
        return cw[0:1] * before + cw[1:2] * mid + cw[2:3] * after + cb_ref[:, cols]

    chunks = [slice(s * sub, (s + 1) * sub) for s in range(wd_ref.shape[0] // sub)]
    up_proj = lambda cols: (jnp.dot(h, wu_ref[:, cols], preferred_element_type=F32),
                            jnp.dot(h, wg_ref[:, cols], preferred_element_type=F32))
    ups = [up_proj(chunks[0])]
    down = None
    for s, cols in enumerate(chunks):
        if s + 1 < len(chunks):
            ups.append(up_proj(chunks[s + 1]))
        cu = conv3(ups[s][0], cwu_ref, cbu_ref, cols)
        cg = conv3(ups[s][1], cwg_ref, cbg_ref, cols)
        act = (_silu(cg) * cu).astype(BF16)
        part = jnp.dot(act, wd_ref[cols, :], preferred_element_type=F32)
        down = part if down is None else down + part
    o_ref[...] += down

    @pl.when(f == pl.num_programs(1) - 1)
    def _():
        x_new = x_ref[...] + gate_ref[0] * o_ref[...]
        if final_norm:
            ms = jnp.mean(x_new * x_new, axis=-1, keepdims=True)
            x_new = x_new * lax.rsqrt(ms + EPS) * gout_ref[...]
        o_ref[...] = x_new


def _conv_ffn(x2d, g, mod, mod_row, w_up, conv_w, conv_b, w_down, g_out, seq_len, tm, tf, final_norm):
    n, d = x2d.shape
    ffn = w_down.shape[0]
    nf = ffn // tf
    halo = 8
    nh = n // halo
    return pl.pallas_call(
        functools.partial(_ffn_kernel, tm=tm, sub=min(tf, FFN_SUB), tiles_per_seq=seq_len // tm,
                          final_norm=final_norm),
        grid=(n // tm, nf),
        in_specs=[
            pl.BlockSpec((tm, d), lambda i, f: (i, 0)),
            pl.BlockSpec((halo, d), lambda i, f: (jnp.maximum(i * (tm // halo) - 1, 0), 0)),
            pl.BlockSpec((halo, d), lambda i, f: (jnp.minimum((i + 1) * (tm // halo), nh - 1), 0)),
            pl.BlockSpec((1, d), lambda i, f: (0, 0)),
            pl.BlockSpec((1, 1, d), lambda i, f: (mod_row(i) + 3, 0, 0)),
            pl.BlockSpec((1, 1, d), lambda i, f: (mod_row(i) + 4, 0, 0)),
            pl.BlockSpec((1, 1, d), lambda i, f: (mod_row(i) + 5, 0, 0)),
            pl.BlockSpec((d, tf), lambda i, f: (0, f)),
            pl.BlockSpec((d, tf), lambda i, f: (0, nf + f)),
            pl.BlockSpec((3, tf), lambda i, f: (0, f)),
            pl.BlockSpec((3, tf), lambda i, f: (0, nf + f)),
            pl.BlockSpec((1, tf), lambda i, f: (0, f)),
            pl.BlockSpec((1, tf), lambda i, f: (0, nf + f)),
            pl.BlockSpec((tf, d), lambda i, f: (f, 0)),
            pl.BlockSpec((1, d), lambda i, f: (0, 0)),
        ],
        out_specs=pl.BlockSpec((tm, d), lambda i, f: (i, 0)),
        out_shape=jax.ShapeDtypeStruct((n, d), F32),
        scratch_shapes=[pltpu.VMEM((tm + 16, d), BF16)],
        compiler_params=_params("arbitrary", "arbitrary"),
        name="conv_ffn",
    )(x2d, x2d, x2d, g.reshape(1, d), mod, mod, mod, w_up, w_up, conv_w, conv_w,
      conv_b.reshape(1, -1), conv_b.reshape(1, -1), w_down, g_out.reshape(1, d))


def _pw1_glu_kernel(x_ref, g_ref, shift_ref, scale_ref, wa_ref, wg_ref, z_ref, h_scr):
    @pl.when(pl.program_id(1) == 0)
    def _():
        h_scr[...] = _rms_mod(x_ref[...], g_ref[...], shift_ref[0], scale_ref[0]).astype(BF16)

    h = h_scr[...]
    a = jnp.dot(h, wa_ref[...], preferred_element_type=F32)
    gate = jnp.dot(h, wg_ref[...], preferred_element_type=F32)
    z_ref[...] = a * jax.nn.sigmoid(gate)


def _pw1_glu(x2d, g, mod, mod_row, w_pw1, tm, tn):
    n, d = x2d.shape
    width = w_pw1.shape[1] // 2
    nj = width // tn
    return pl.pallas_call(
        _pw1_glu_kernel,
        grid=(n // tm, nj),
        in_specs=[
            pl.BlockSpec((tm, d), lambda i, j: (i, 0)),
            pl.BlockSpec((1, d), lambda i, j: (0, 0)),
            pl.BlockSpec((1, 1, d), lambda i, j: (mod_row(i) + 0, 0, 0)),
            pl.BlockSpec((1, 1, d), lambda i, j: (mod_row(i) + 1, 0, 0)),
            pl.BlockSpec((d, tn), lambda i, j: (0, j)),
            pl.BlockSpec((d, tn), lambda i, j: (0, nj + j)),
        ],
        out_specs=pl.BlockSpec((tm, tn), lambda i, j: (i, j)),
        out_shape=jax.ShapeDtypeStruct((n, width), F32),
        scratch_shapes=[pltpu.VMEM((tm, d), BF16)],
        compiler_params=_params("arbitrary", "arbitrary"),
        name="pw1_glu",
    )(x2d, g.reshape(1, d), mod, mod, w_pw1, w_pw1)


def _dwconv_ln_pw2_kernel(z_ref, zp_ref, zn_ref, dw_ref, db_ref, lg_ref, lb_ref, w2_ref, x_ref, gate_ref,
                          o_ref, zs_scr, cv_scr, sh_scr, *, tm, tiles_per_seq, row_blk):
    i = pl.program_id(0)
    pad = CONV_K // 2 + 1
    first = (i % tiles_per_seq) == 0
    last = (i % tiles_per_seq) == tiles_per_seq - 1
    zs_scr[0:pad] = jnp.where(first, 0.0, zp_ref[...])
    zs_scr[pad:pad + tm] = z_ref[...]
    zs_scr[pad + tm:2 * pad + tm] = jnp.where(last, 0.0, zn_ref[...])

    lanes = 128
    sublanes = 8
    span = tm + (CONV_K - 1) // sublanes * sublanes

    def conv_cols(c, carry):
        cols = pl.ds(pl.multiple_of(c * lanes, lanes), lanes)
        for b in range(sublanes):
            sh_scr[b] = zs_scr[pl.ds(1 + b, span), cols]
        for rb in range(tm // row_blk):
            acc = jnp.zeros((row_blk, lanes), F32) + db_ref[:, cols]
            for k in range(CONV_K):
                off = rb * row_blk + k // sublanes * sublanes
                acc = acc + sh_scr[k % sublanes, pl.ds(off, row_blk), :] * dw_ref[k:k + 1, cols]
            cv_scr[rb * row_blk:(rb + 1) * row_blk, cols] = acc
        return carry

    lax.fori_loop(0, cv_scr.shape[1] // lanes, conv_cols, 0)

    v = cv_scr[...]
    mu = jnp.mean(v, axis=-1, keepdims=True)
    var = jnp.mean(jnp.square(v - mu), axis=-1, keepdims=True)
    y = _silu((v - mu) * lax.rsqrt(var + EPS) * lg_ref[...] + lb_ref[...])
    o = jnp.dot(y.astype(BF16), w2_ref[...], preferred_element_type=F32)
    o_ref[...] = x_ref[...] + gate_ref[0] * o


def _dwconv_ln_pw2(z, x2d, dw_w, dw_b, ln_g, ln_b, w_pw2, mod, mod_row, seq_len, tm):
    n, width = z.shape
    d = x2d.shape[1]
    pad = CONV_K // 2 + 1
    nh = n // pad
    row = lambda t: t.reshape(1, -1).astype(F32)
    return pl.pallas_call(
        functools.partial(_dwconv_ln_pw2_kernel, tm=tm, tiles_per_seq=seq_len // tm, row_blk=64),
        grid=(n // tm,),
        in_specs=[
            pl.BlockSpec((tm, width), lambda i: (i, 0)),
            pl.BlockSpec((pad, width), lambda i: (jnp.maximum(i * (tm // pad) - 1, 0), 0)),
            pl.BlockSpec((pad, width), lambda i: (jnp.minimum((i + 1) * (tm // pad), nh - 1), 0)),
            pl.BlockSpec((CONV_K, width), lambda i: (0, 0)),
            pl.BlockSpec((1, width), lambda i: (0, 0)),
            pl.BlockSpec((1, width), lambda i: (0, 0)),
            pl.BlockSpec((1, width), lambda i: (0, 0)),
            pl.BlockSpec((width, d), lambda i: (0, 0)),
            pl.BlockSpec((tm, d), lambda i: (i, 0)),
            pl.BlockSpec((1, 1, d), lambda i: (mod_row(i) + 2, 0, 0)),
        ],
        out_specs=pl.BlockSpec((tm, d), lambda i: (i, 0)),
        out_shape=jax.ShapeDtypeStruct((n, d), F32),
        scratch_shapes=[pltpu.VMEM((tm + 2 * pad, width), F32), pltpu.VMEM((tm, width), F32),
                        pltpu.VMEM((8, tm + (CONV_K - 1) // 8 * 8, 128), F32)],
        compiler_params=_params("arbitrary"),
        name="dwconv_ln_pw2",
    )(z, z, z, dw_w.astype(F32), row(dw_b), row(ln_g), row(ln_b), w_pw2, x2d, mod)


def kernel(x, c, ctx, c_ctx, w_mod, b_mod, g_mix, g_ffn, w_in, ssm_lam_re, ssm_lam_im, ssm_log_dt, ssm_b_re, ssm_b_im, ssm_c_re, ssm_c_im, ssm_d, ssm_w_glu, na_rpb, w_out, cv_w_pw1, cv_dw_w, cv_dw_b, cv_ln_g, cv_ln_b, cv_w_pw2, ffn_w_up, ffn_conv_w, ffn_conv_b, ffn_w_down, g_out):
    bsz, seq_len, d = x.shape
    ctx_len = ctx.shape[1]
    n = bsz * seq_len
    ssm_width = ssm_d.shape[1]
    assert w_mod.shape[0] == 2 and bsz < 8, "two layers: hybrid mixer, then Conformer convolution"

    c8 = jnp.concatenate([c, c_ctx[None], jnp.zeros((8 - bsz - 1, d), c.dtype)], axis=0).astype(F32)
    mod = _modulation(c8, w_mod, b_mod).reshape(2 * 8 * N_MOD, 1, d)
    tm = 512
    lat_row = lambda layer, tile: (lambda i: (layer * 8 + i // (seq_len // tile)) * N_MOD)
    ctx_row = lambda i: bsz * N_MOD

    x2d = x.reshape(n, d)
    ctx2d = ctx.reshape(bsz * ctx_len, d)

    w_in0 = w_in[0].astype(BF16)
    u, q, k, v = _in_proj(x2d, g_mix[0], mod, lat_row(0, tm), w_in0, tm)
    uc, _, kc, vc = _in_proj(ctx2d, g_mix[0], mod, ctx_row, w_in0, ctx_len)

    mats = _s5_matrices(ssm_lam_re[0], ssm_lam_im[0], ssm_log_dt[0], ssm_b_re[0], ssm_b_im[0],
                        ssm_c_re[0], ssm_c_im[0])
    ys = _from_s5_layout(_s5_scan(_to_s5_layout(u, bsz), _to_s5_layout(uc, bsz), mats, bsz))

    to3 = lambda t, length: t.reshape(bsz, length, t.shape[-1])
    yna = _na_attention(to3(q, seq_len), to3(k, seq_len), to3(v, seq_len), to3(kc, ctx_len), to3(vc, ctx_len),
                        _na_bias_mask(na_rpb[0], seq_len // GRID_W))
    x2d = _out_proj(x2d, u, ys, yna.reshape(n, -1), ssm_d[0], ssm_w_glu[0].astype(BF16),
                    w_out[0].astype(BF16), mod, lat_row(0, 256), 256)
    x2d = _conv_ffn(x2d, g_ffn[0], mod, lat_row(0, tm), ffn_w_up[0].astype(BF16), ffn_conv_w[0].astype(F32),
                    ffn_conv_b[0].astype(F32), ffn_w_down[0].astype(BF16), g_out, seq_len, tm, 512, False)

    z = _pw1_glu(x2d, g_mix[1], mod, lat_row(1, tm), cv_w_pw1[0].astype(BF16), tm, min(1024, d))
    x2d = _dwconv_ln_pw2(z, x2d, cv_dw_w[0], cv_dw_b[0], cv_ln_g[0], cv_ln_b[0], cv_w_pw2[0].astype(BF16),
                         mod, lat_row(1, 256), seq_len, 256)
    x2d = _conv_ffn(x2d, g_ffn[1], mod, lat_row(1, tm), ffn_w_up[1].astype(BF16), ffn_conv_w[1].astype(F32),
                    ffn_conv_b[1].astype(F32), ffn_w_down[1].astype(BF16), g_out, seq_len, tm, 512, True)
    return x2d.reshape(bsz, seq_len, d).astype(x.dtype)
```
